```python
import jax, jax.numpy as jnp
from jax import lax
import numpy as np

D_MODEL = 2048
BATCH = 4
SEQ = 8192
DEPTH = 4

PLE_DIM = 256
ROPE_BASE = 10000.0
NORM_EPS = 1e-6
RET_HEADS = 8
RET_DK = 256
RET_DV = 512
RET_QK = RET_HEADS * RET_DK
RET_V = RET_HEADS * RET_DV
RET_IN = 2 * RET_QK + 2 * RET_V
RET_CHUNK = 128
MLA_HEADS = 16
MLA_Q_LORA = 512
MLA_KV_LORA = 512
MLA_NOPE = 128
MLA_ROPE = 64
MLA_V = 128
MLA_IN = MLA_Q_LORA + MLA_KV_LORA + MLA_ROPE
MLA_QBLOCK = 128
D_FF = 5632
N_EXPERTS = 8
TOP_K = 2
MOE_BLOCK = 256

kernel_name = 'hybrid_retention_mla_moe_encoder'


def rms_norm(x, gain):
    xf = x.astype(jnp.float32)
    y = xf * lax.rsqrt(jnp.mean(xf * xf, axis=-1, keepdims=True) + NORM_EPS)
    return (y * gain.astype(jnp.float32)).astype(x.dtype)


def rope(t, pos):
    d = t.shape[-1]
    inv = ROPE_BASE ** (-jnp.arange(0, d, 2, dtype=jnp.float32) / d)
    ang = pos.astype(jnp.float32)[:, None] * inv[None, :]
    cos = jnp.cos(ang)[:, None, :]
    sin = jnp.sin(ang)[:, None, :]
    t1, t2 = jnp.split(t.astype(jnp.float32), 2, axis=-1)
    return jnp.concatenate([t1 * cos - t2 * sin, t1 * sin + t2 * cos], axis=-1).astype(t.dtype)


def _chunk_scan(q, k, v, xi, zeta, decay_c, reverse):
    nb, hh, dk, dv = q.shape[1], q.shape[3], q.shape[4], v.shape[4]

    def step(state, qkv):
        qc, kc, vc = qkv
        out = jnp.einsum('bqhd,bhde->bqhe', qc * xi[:, :, None], state)
        state = state * decay_c[None, :, None, None] + jnp.einsum('bkhd,bkhe->bhde', kc * zeta[:, :, None], vc)
        return state, out

    init = jnp.zeros((nb, hh, dk, dv), jnp.float32)
    _, out = lax.scan(step, init, (q, k, v), reverse=reverse)
    return out


def retention(xn, w_in, log_decay, w_out, pos):
    b, s, _ = xn.shape
    nc, c = s // RET_CHUNK, RET_CHUNK
    w_q, w_k, w_v, w_g = jnp.split(w_in, [RET_QK, 2 * RET_QK, 2 * RET_QK + RET_V], axis=1)
    q = rope((xn @ w_q).reshape(b, s, RET_HEADS, RET_DK), pos)
    k = rope((xn @ w_k).reshape(b, s, RET_HEADS, RET_DK), pos) * (RET_DK ** -0.5)
    v = (xn @ w_v).reshape(b, s, RET_HEADS, RET_DV)
    g = xn @ w_g
    lf = log_decay[0].astype(jnp.float32)
    lb = log_decay[1].astype(jnp.float32)
    ci = jnp.arange(c, dtype=jnp.float32)
    dist = ci[:, None] - ci[None, :]
    mask = jnp.where(dist >= 0,
                     jnp.exp(lf[:, None, None] * jnp.maximum(dist, 0.0)),
                     jnp.exp(lb[:, None, None] * jnp.maximum(-dist, 0.0)))
    qc = q.reshape(b, nc, c, RET_HEADS, RET_DK)
    kc = k.reshape(b, nc, c, RET_HEADS, RET_DK)
    vc = v.reshape(b, nc, c, RET_HEADS, RET_DV)
    scores = jnp.einsum('bnqhd,bnkhd->bnhqk', qc, kc, preferred_element_type=jnp.float32) * mask
    y = jnp.einsum('bnhqk,bnkhe->bnqhe', scores, vc)
    xi_f = jnp.exp(lf[None, :] * (ci[:, None] + 1.0))
    zeta_f = jnp.exp(lf[None, :] * (c - 1.0 - ci[:, None]))
    xi_b = jnp.exp(lb[None, :] * (c - ci[:, None]))
    zeta_b = jnp.exp(lb[None, :] * ci[:, None])
    qs, ks, vs = (a.transpose(1, 0, 2, 3, 4) for a in (qc, kc, vc))
    y_f = _chunk_scan(qs, ks, vs, xi_f, zeta_f, jnp.exp(lf * c), False)
    y_b = _chunk_scan(qs, ks, vs, xi_b, zeta_b, jnp.exp(lb * c), True)
    y = (y + (y_f + y_b).transpose(1, 0, 2, 3, 4)).reshape(b, s, RET_HEADS, RET_DV)
    y = y * lax.rsqrt(jnp.mean(y * y, axis=-1, keepdims=True) + NORM_EPS)
    o = jax.nn.silu(g.astype(jnp.float32)).reshape(b, s, RET_HEADS, RET_DV) * y
    return o.reshape(b, s, RET_V).astype(xn.dtype) @ w_out


def mla(xn, w_in, q_norm, kv_norm, w_uq, w_ukv, w_o, pos):
    b, s, _ = xn.shape
    w_cq, w_ckv, w_kr = jnp.split(w_in, [MLA_Q_LORA, MLA_Q_LORA + MLA_KV_LORA], axis=1)
    c_q = rms_norm(xn @ w_cq, q_norm)
    c_kv = rms_norm(xn @ w_ckv, kv_norm)
    k_r = rope((xn @ w_kr)[:, :, None, :], pos)[:, :, 0, :]
    q = (c_q @ w_uq).reshape(b, s, MLA_HEADS, MLA_NOPE + MLA_ROPE)
    q_n, q_r = q[..., :MLA_NOPE], rope(q[..., MLA_NOPE:], pos)
    kv = (c_kv @ w_ukv).reshape(b, s, MLA_HEADS, MLA_NOPE + MLA_V)
    k_n, v = kv[..., :MLA_NOPE], kv[..., MLA_NOPE:]
    scale = (MLA_NOPE + MLA_ROPE) ** -0.5
    nq = s // MLA_QBLOCK
    qn_b = q_n.reshape(b, nq, MLA_QBLOCK, MLA_HEADS, MLA_NOPE).transpose(1, 0, 2, 3, 4)
    qr_b = q_r.reshape(b, nq, MLA_QBLOCK, MLA_HEADS, MLA_ROPE).transpose(1, 0, 2, 3, 4)

    def attend(blk):
        qn, qr = blk
        sc = (jnp.einsum('bqhd,bkhd->bhqk', qn, k_n, preferred_element_type=jnp.float32)
              + jnp.einsum('bqhr,bkr->bhqk', qr, k_r, preferred_element_type=jnp.float32)) * scale
        probs = jax.nn.softmax(sc, axis=-1)
        return jnp.einsum('bhqk,bkhd->bqhd', probs.astype(v.dtype), v)

    o = lax.map(attend, (qn_b, qr_b))
    o = o.transpose(1, 0, 2, 3, 4).reshape(b, s, MLA_HEADS * MLA_V)
    return o @ w_o


def swiglu(hn, w_gate, w_up, w_down):
    return (jax.nn.silu(hn @ w_gate) * (hn @ w_up)) @ w_down


def moe_swiglu(hn, w_router, w_gate, w_up, w_down):
    b, s, d = hn.shape
    t = hn.reshape(-1, d)
    n_tok = t.shape[0]
    n_asg = n_tok * TOP_K
    logits = jnp.einsum('td,de->te', t, w_router, preferred_element_type=jnp.float32)
    top_logit, top_idx = lax.top_k(logits, TOP_K)
    top_w = jax.nn.softmax(top_logit, axis=-1)
    flat_e = top_idx.reshape(n_asg)
    flat_tok = jnp.repeat(jnp.arange(n_tok, dtype=jnp.int32), TOP_K)
    flat_w = top_w.reshape(n_asg)
    order = jnp.argsort(flat_e, stable=True)
    e_sorted = flat_e[order]
    counts = jnp.bincount(flat_e, length=N_EXPERTS)
    padded = (counts + MOE_BLOCK - 1) // MOE_BLOCK * MOE_BLOCK
    grp_start = jnp.cumsum(counts) - counts
    pad_end = jnp.cumsum(padded)
    pad_start = pad_end - padded
    dest = pad_start[e_sorted] + jnp.arange(n_asg, dtype=jnp.int32) - grp_start[e_sorted]
    n_blocks = -(-n_asg // MOE_BLOCK) + N_EXPERTS
    n_rows = n_blocks * MOE_BLOCK
    row_tok = jnp.full((n_rows,), n_tok, jnp.int32).at[dest].set(flat_tok[order])
    row_w = jnp.zeros((n_rows,), jnp.float32).at[dest].set(flat_w[order])
    blk_expert = jnp.minimum(jnp.searchsorted(pad_end, jnp.arange(n_blocks) * MOE_BLOCK, side='right'),
                             N_EXPERTS - 1)
    t_pad = jnp.concatenate([t, jnp.zeros((1, d), t.dtype)], axis=0)

    def expert_block(args):
        tok, e = args
        xb = t_pad[tok]
        return (jax.nn.silu(xb @ w_gate[e]) * (xb @ w_up[e])) @ w_down[e]

    yb = lax.map(expert_block, (row_tok.reshape(n_blocks, MOE_BLOCK), blk_expert))
    y = yb.reshape(n_rows, d) * row_w[:, None].astype(yb.dtype)
    out = jnp.zeros((n_tok + 1, d), y.dtype).at[row_tok].add(y)[:n_tok]
    return out.reshape(b, s, d)


def setup_inputs(seed: int = 0) -> dict:
    key = jax.random.key(seed)
    ks = iter(jax.random.split(key, 40))
    f32 = jnp.float32

    def nrm(shape, scale):
        return jax.random.normal(next(ks), shape, f32) * scale

    def gain(shape):
        return 1.0 + 0.01 * jax.random.normal(next(ks), shape, f32)

    n_a = (DEPTH + 1) // 2
    n_b = DEPTH // 2
    base_ld = jnp.log(1.0 - 2.0 ** (-5.0 - jnp.arange(RET_HEADS, dtype=f32)))
    ret_log_decay = base_ld[None, None, :] * jnp.exp(0.1 * jax.random.normal(next(ks), (n_a, 2, RET_HEADS), f32))
    return {
        'x': nrm((BATCH, SEQ, D_MODEL), 1.0),
        'p': nrm((DEPTH, BATCH, SEQ, PLE_DIM), 1.0),
        'mix_norm': gain((DEPTH, D_MODEL)),
        'ffn_norm': gain((DEPTH, D_MODEL)),
        'ple_norm': gain((DEPTH, D_MODEL)),
        'ret_w_in': nrm((n_a, D_MODEL, RET_IN), D_MODEL ** -0.5),
        'ret_log_decay': ret_log_decay,
        'ret_w_out': nrm((n_a, RET_V, D_MODEL), RET_V ** -0.5),
        'mla_w_in': nrm((n_b, D_MODEL, MLA_IN), D_MODEL ** -0.5),
        'mla_q_norm': gain((n_b, MLA_Q_LORA)),
        'mla_kv_norm': gain((n_b, MLA_KV_LORA)),
        'mla_w_uq': nrm((n_b, MLA_Q_LORA, MLA_HEADS * (MLA_NOPE + MLA_ROPE)), MLA_Q_LORA ** -0.5),
        'mla_w_ukv': nrm((n_b, MLA_KV_LORA, MLA_HEADS * (MLA_NOPE + MLA_V)), MLA_KV_LORA ** -0.5),
        'mla_w_o': nrm((n_b, MLA_HEADS * MLA_V, D_MODEL), (MLA_HEADS * MLA_V) ** -0.5),
        'ffn_w_gate': nrm((n_a, D_MODEL, D_FF), D_MODEL ** -0.5),
        'ffn_w_up': nrm((n_a, D_MODEL, D_FF), D_MODEL ** -0.5),
        'ffn_w_down': nrm((n_a, D_FF, D_MODEL), D_FF ** -0.5),
        'moe_w_router': nrm((n_b, D_MODEL, N_EXPERTS), D_MODEL ** -0.5),
        'moe_w_gate': nrm((n_b, N_EXPERTS, D_MODEL, D_FF), D_MODEL ** -0.5),
        'moe_w_up': nrm((n_b, N_EXPERTS, D_MODEL, D_FF), D_MODEL ** -0.5),
        'moe_w_down': nrm((n_b, N_EXPERTS, D_FF, D_MODEL), D_FF ** -0.5),
        'ple_w_up': nrm((DEPTH, PLE_DIM, D_MODEL), PLE_DIM ** -0.5),
        'ple_w_gate': nrm((DEPTH, D_MODEL, D_MODEL), D_MODEL ** -0.5),
        'final_norm': gain((D_MODEL,)),
    }


def reference(x, p, mix_norm, ffn_norm, ple_norm, ret_w_in, ret_log_decay, ret_w_out,
              mla_w_in, mla_q_norm, mla_kv_norm, mla_w_uq, mla_w_ukv, mla_w_o,
              ffn_w_gate, ffn_w_up, ffn_w_down, moe_w_router, moe_w_gate, moe_w_up, moe_w_down,
              ple_w_up, ple_w_gate, final_norm):
    pos = jnp.arange(x.shape[1], dtype=jnp.int32)
    h = x
    for i in range(DEPTH):
        j = i // 2
        hn = rms_norm(h, mix_norm[i])
        if i % 2 == 0:
            mix = retention(hn, ret_w_in[j], ret_log_decay[j], ret_w_out[j], pos)
        else:
            mix = mla(hn, mla_w_in[j], mla_q_norm[j], mla_kv_norm[j], mla_w_uq[j], mla_w_ukv[j], mla_w_o[j], pos)
        h = h + mix.astype(h.dtype)
        hn = rms_norm(h, ffn_norm[i])
        if i % 2 == 0:
            ff = swiglu(hn, ffn_w_gate[j], ffn_w_up[j], ffn_w_down[j])
        else:
            ff = moe_swiglu(hn, moe_w_router[j], moe_w_gate[j], moe_w_up[j], moe_w_down[j])
        h = h + ff.astype(h.dtype)
        gate = jax.nn.sigmoid(rms_norm(h, ple_norm[i]) @ ple_w_gate[i])
        h = h + ((p[i] @ ple_w_up[i]) * gate).astype(h.dtype)
    return rms_norm(h, final_norm)
```

```python
import functools

import jax
import jax.numpy as jnp
from jax import lax
from jax.experimental import pallas as pl
from jax.experimental.pallas import tpu as pltpu

F32 = jnp.float32
BF16 = jnp.bfloat16

NORM_EPS = 1e-6
ROPE_BASE = 10000.0
RET_HEADS = 8
RET_DK = 256
RET_DV = 512
RET_CHUNK = 256
MLA_HEADS = 16
MLA_Q_LORA = 512
MLA_KV_LORA = 512
MLA_NOPE = 128
MLA_ROPE = 64
MLA_V = 128
MLA_QK_PAD = 256
N_EXPERTS = 8
TOP_K = 2
LANES = 128
V7X_VMEM_LIMIT = 56 * 1024 * 1024


def _params(*sem):
    return pltpu.CompilerParams(dimension_semantics=sem, vmem_limit_bytes=V7X_VMEM_LIMIT)


def _rms(x, gain):
    return x * lax.rsqrt(jnp.mean(x * x, axis=-1, keepdims=True) + NORM_EPS) * gain


def _sigmoid(x):
    return 1.0 / (1.0 + jnp.exp(-x))


def _dot(a, b):
    return jnp.dot(a, b, preferred_element_type=F32)


def _rmsnorm_kernel(x_ref, g_ref, o_ref):
    o_ref[...] = _rms(x_ref[...], g_ref[...]).astype(o_ref.dtype)


def _rmsnorm(x, gain, out_dtype, tm=512):
    m, d = x.shape
    tm = min(tm, m)
    return pl.pallas_call(
        _rmsnorm_kernel,
        out_shape=jax.ShapeDtypeStruct((m, d), out_dtype),
        grid=(m // tm,),
        in_specs=[pl.BlockSpec((tm, d), lambda i: (i, 0)), pl.BlockSpec((1, d), lambda i: (0, 0))],
        out_specs=pl.BlockSpec((tm, d), lambda i: (i, 0)),
        compiler_params=_params("parallel"),
        name="rmsnorm",
    )(x, gain.reshape(1, d))


def _mm_kernel(*refs, n_w, n_extra, epilogue):
    x = refs[0][...]
    accs = [_dot(x, w[...]) for w in refs[1:1 + n_w]]
    extras = refs[1 + n_w:1 + n_w + n_extra]
    outs = refs[1 + n_w + n_extra:]
    epilogue(accs, extras, outs)


def _mm(x, ws, epilogue, *, tm, tn, extras=(), extra_specs=(), name):
    m, k = x.shape
    n = ws[0].shape[1]
    tm, tn = min(tm, m), min(tn, n)
    kern = functools.partial(_mm_kernel, n_w=len(ws), n_extra=len(extras), epilogue=epilogue)
    return pl.pallas_call(
        kern,
        out_shape=jax.ShapeDtypeStruct((m, n), BF16),
        grid=(m // tm, n // tn),
        in_specs=[pl.BlockSpec((tm, k), lambda i, j: (i, 0))]
        + [pl.BlockSpec((k, tn), lambda i, j: (0, j))] * len(ws) + list(extra_specs),
        out_specs=pl.BlockSpec((tm, tn), lambda i, j: (i, j)),
        compiler_params=_params("parallel", "arbitrary"),
        name=name,
    )(x, *ws, *extras)


def _epi_plain(accs, extras, outs):
    outs[0][...] = accs[0].astype(outs[0].dtype)


def _epi_swiglu(accs, extras, outs):
    a, b = accs
    outs[0][...] = (a * _sigmoid(a) * b).astype(outs[0].dtype)


def _epi_rope_heads(accs, extras, outs):
    cos, sin, scale = extras[0][...], extras[1][...], extras[2][...]
    a = accs[0]
    half = RET_DK // 2
    for hd in range(a.shape[1] // RET_DK):
        lo = hd * RET_DK
        t1, t2 = a[:, lo:lo + half], a[:, lo + half:lo + RET_DK]
        outs[0][:, lo:lo + half] = ((t1 * cos - t2 * sin) * scale[:, lo:lo + half]).astype(outs[0].dtype)
        outs[0][:, lo + half:lo + RET_DK] = (
            (t1 * sin + t2 * cos) * scale[:, lo + half:lo + RET_DK]).astype(outs[0].dtype)


def _epi_tile_rmsnorm(accs, extras, outs):
    outs[0][...] = _rms(accs[0], extras[0][...]).astype(outs[0].dtype)


def _epi_cos_sin(accs, extras, outs):
    a, b = accs
    outs[0][...] = (a * extras[0][...] + b * extras[1][...]).astype(outs[0].dtype)


def _mm_res_norm_kernel(x_ref, w_ref, h_ref, g_ref, h_out_ref, hn_out_ref, acc_ref):
    k = pl.program_id(1)

    @pl.when(k == 0)
    def _():
        acc_ref[...] = jnp.zeros_like(acc_ref)

    acc_ref[...] += _dot(x_ref[...], w_ref[...])

    @pl.when(k == pl.num_programs(1) - 1)
    def _():
        h_new = h_ref[...] + acc_ref[...]
        h_out_ref[...] = h_new
        hn_out_ref[...] = _rms(h_new, g_ref[...]).astype(hn_out_ref.dtype)


def _mm_res_norm(x, w, h, gain, *, tm=512, tk=512, name):
    m, kdim = x.shape
    n = w.shape[1]
    tm, tk = min(tm, m), min(tk, kdim)
    return pl.pallas_call(
        _mm_res_norm_kernel,
        out_shape=(jax.ShapeDtypeStruct((m, n), F32), jax.ShapeDtypeStruct((m, n), BF16)),
        grid=(m // tm, kdim // tk),
        in_specs=[
            pl.BlockSpec((tm, tk), lambda i, k: (i, k)),
            pl.BlockSpec((tk, n), lambda i, k: (k, 0)),
            pl.BlockSpec((tm, n), lambda i, k: (i, 0)),
            pl.BlockSpec((1, n), lambda i, k: (0, 0)),
        ],
        out_specs=(pl.BlockSpec((tm, n), lambda i, k: (i, 0)), pl.BlockSpec((tm, n), lambda i, k: (i, 0))),
        scratch_shapes=[pltpu.VMEM((tm, n), F32)],
        compiler_params=_params("parallel", "arbitrary"),
        name=name,
    )(x, w, h, gain.reshape(1, n))


def _ple_kernel(*refs, add_ff, emit_h):
    if add_ff:
        h_ref, ff_ref, gp_ref, wg_ref, p_ref, wu_ref, gn_ref = refs[:7]
        outs = refs[7:]
        h = h_ref[...] + ff_ref[...].astype(F32)
        hn = _rms(h, gp_ref[...]).astype(BF16)
    else:
        h_ref, hn_ref, wg_ref, p_ref, wu_ref, gn_ref = refs[:6]
        outs = refs[6:]
        h = h_ref[...]
        hn = hn_ref[...]
    gate = _sigmoid(_dot(hn, wg_ref[...]))
    up = _dot(p_ref[...].astype(BF16), wu_ref[...])
    h_new = h + up * gate
    if emit_h:
        outs[0][...] = h_new
    outs[-1][...] = _rms(h_new, gn_ref[...]).astype(outs[-1].dtype)


def _ple(h, hn_or_ff, ple_gain, w_gate, p, w_up, next_gain, *, add_ff, emit_h, tm=256):
    m, d = h.shape
    pd = p.shape[1]
    tm = min(tm, m)
    row = lambda i: (i, 0)
    const = lambda i: (0, 0)
    ins = [h, hn_or_ff]
    specs = [pl.BlockSpec((tm, d), row), pl.BlockSpec((tm, d), row)]
    if add_ff:
        ins.append(ple_gain.reshape(1, d))
        specs.append(pl.BlockSpec((1, d), const))
    ins += [w_gate, p, w_up, next_gain.reshape(1, d)]
    specs += [pl.BlockSpec((d, d), const), pl.BlockSpec((tm, pd), row), pl.BlockSpec((pd, d), const),
              pl.BlockSpec((1, d), const)]
    out_dtype = BF16 if emit_h else F32
    out_shape = [jax.ShapeDtypeStruct((m, d), out_dtype)]
    out_specs = [pl.BlockSpec((tm, d), row)]
    if emit_h:
        out_shape.insert(0, jax.ShapeDtypeStruct((m, d), F32))
        out_specs.insert(0, pl.BlockSpec((tm, d), row))
    res = pl.pallas_call(
        functools.partial(_ple_kernel, add_ff=add_ff, emit_h=emit_h),
        out_shape=tuple(out_shape),
        grid=(m // tm,),
        in_specs=specs,
        out_specs=tuple(out_specs),
        compiler_params=_params("parallel"),
        name="ple",
    )(*ins)
    return res if emit_h else (None, res[0])


def _ret_scan_step(q, k, v, sc_ref, dec, state_ref, xi_col, zeta_col):
    xi = sc_ref[0, :, xi_col:xi_col + 1]
    zeta = sc_ref[0, :, zeta_col:zeta_col + 1]
    out = _dot((q.astype(F32) * xi).astype(BF16), state_ref[...].astype(BF16))
    kz = (k.astype(F32) * zeta).astype(BF16)
    state_ref[...] = state_ref[...] * dec + lax.dot_general(
        kz, v, (((0,), (0,)), ((), ())), preferred_element_type=F32)
    return out


def _ret_bwd_kernel(dec_ref, q_ref, k_ref, v_ref, sc_ref, yb_ref, state_ref):
    @pl.when(pl.program_id(2) == 0)
    def _():
        state_ref[...] = jnp.zeros_like(state_ref)

    dec = dec_ref[1, pl.program_id(1)]
    out = _ret_scan_step(q_ref[...], k_ref[...], v_ref[...], sc_ref, dec, state_ref, 2, 3)
    yb_ref[...] = out.astype(yb_ref.dtype)


def _ret_fwd_kernel(dec_ref, q_ref, k_ref, v_ref, g_ref, yb_ref, mask_ref, sc_ref, o_ref, state_ref):
    @pl.when(pl.program_id(2) == 0)
    def _():
        state_ref[...] = jnp.zeros_like(state_ref)

    q, k, v = q_ref[...], k_ref[...], v_ref[...]
    scores = lax.dot_general(q, k, (((1,), (1,)), ((), ())), preferred_element_type=F32) * mask_ref[0]
    y = _dot(scores.astype(BF16), v) + yb_ref[...].astype(F32)
    dec = dec_ref[0, pl.program_id(1)]
    y = y + _ret_scan_step(q, k, v, sc_ref, dec, state_ref, 0, 1)
    y = y * lax.rsqrt(jnp.mean(y * y, axis=-1, keepdims=True) + NORM_EPS)
    g = g_ref[...].astype(F32)
    o_ref[...] = (g * _sigmoid(g) * y).astype(o_ref.dtype)


def _retention_core(qk, vg, log_decay, batch, seq):
    t = qk.shape[0]
    h, dk, dv, c = RET_HEADS, RET_DK, RET_DV, min(RET_CHUNK, seq)
    nc = seq // c
    lf, lb = log_decay[0].astype(F32), log_decay[1].astype(F32)
    ci = jnp.arange(c, dtype=F32)
    dist = ci[:, None] - ci[None, :]
    mask = jnp.where(dist >= 0,
                     jnp.exp(lf[:, None, None] * jnp.maximum(dist, 0.0)),
                     jnp.exp(lb[:, None, None] * jnp.maximum(-dist, 0.0)))
    scales = jnp.stack([jnp.exp(lf[:, None] * (ci[None, :] + 1.0)),
                        jnp.exp(lf[:, None] * (c - 1.0 - ci[None, :])),
                        jnp.exp(lb[:, None] * (c - ci[None, :])),
                        jnp.exp(lb[:, None] * ci[None, :])], axis=-1)
    dec = jnp.stack([jnp.exp(lf * c), jnp.exp(lb * c)])

    smem = pl.BlockSpec(memory_space=pltpu.SMEM)
    sc_spec = pl.BlockSpec((1, c, 4), lambda b, hh, n: (hh, 0, 0))

    def chunk_specs(chunk_of):
        row = lambda b, n: b * nc + chunk_of(n)
        return dict(
            q=pl.BlockSpec((c, dk), lambda b, hh, n: (row(b, n), hh)),
            k=pl.BlockSpec((c, dk), lambda b, hh, n: (row(b, n), h + hh)),
            v=pl.BlockSpec((c, dv), lambda b, hh, n: (row(b, n), hh)),
            g=pl.BlockSpec((c, dv), lambda b, hh, n: (row(b, n), h + hh)),
            y=pl.BlockSpec((c, dv), lambda b, hh, n: (row(b, n), hh)),
        )

    rev = chunk_specs(lambda n: nc - 1 - n)
    yb = pl.pallas_call(
        _ret_bwd_kernel,
        out_shape=jax.ShapeDtypeStruct((t, h * dv), BF16),
        grid=(batch, h, nc),
        in_specs=[smem, rev["q"], rev["k"], rev["v"], sc_spec],
        out_specs=rev["y"],
        scratch_shapes=[pltpu.VMEM((dk, dv), F32)],
        compiler_params=_params("parallel", "parallel", "arbitrary"),
        name="retention_backward_scan",
    )(dec, qk, qk, vg, scales)

    fwd = chunk_specs(lambda n: n)
    return pl.pallas_call(
        _ret_fwd_kernel,
        out_shape=jax.ShapeDtypeStruct((t, h * dv), BF16),
        grid=(batch, h, nc),
        in_specs=[smem, fwd["q"], fwd["k"], fwd["v"], fwd["g"], fwd["y"],
                  pl.BlockSpec((1, c, c), lambda b, hh, n: (hh, 0, 0)), sc_spec],
        out_specs=fwd["y"],
        scratch_shapes=[pltpu.VMEM((dk, dv), F32)],
        compiler_params=_params("parallel", "parallel", "arbitrary"),
        name="retention_forward",
    )(dec, qk, qk, vg, vg, yb, mask, scales)


def _rope_tables(seq, dim):
    inv = ROPE_BASE ** (-jnp.arange(0, dim, 2, dtype=F32) / dim)
    ang = jnp.arange(seq, dtype=F32)[:, None] * inv[None, :]
    return jnp.cos(ang), jnp.sin(ang)


def _rotate_half_cols(w, dim):
    k, n = w.shape
    wg = w.reshape(k, n // dim, 2, dim // 2)
    return jnp.stack([-wg[:, :, 1], wg[:, :, 0]], axis=2).reshape(k, n)


def _retention_layer(h, hn, w_in, log_decay, w_out, next_gain, batch, seq):
    nqk = 2 * RET_HEADS * RET_DK
    w_qk = w_in[:, :nqk].astype(BF16)
    w_vg = w_in[:, nqk:].astype(BF16)
    cos, sin = _rope_tables(seq, RET_DK)
    col_scale = jnp.concatenate([jnp.ones((nqk // 2,), F32), jnp.full((nqk // 2,), RET_DK ** -0.5, F32)])
    tm = min(1024, seq)
    n_pos = seq // tm
    half = RET_DK // 2
    qk = _mm(hn, [w_qk], _epi_rope_heads, tm=tm, tn=512,
             extras=(cos, sin, col_scale.reshape(1, nqk)),
             extra_specs=(pl.BlockSpec((tm, half), lambda i, j: (i % n_pos, 0)),
                          pl.BlockSpec((tm, half), lambda i, j: (i % n_pos, 0)),
                          pl.BlockSpec((1, 512), lambda i, j: (0, j))),
             name="retention_qk_proj")
    vg = _mm(hn, [w_vg], _epi_plain, tm=tm, tn=512, name="retention_vg_proj")
    o = _retention_core(qk, vg, log_decay, batch, seq)
    return _mm_res_norm(o, w_out.astype(BF16), h, next_gain, name="retention_out_proj")


def _flash_kernel(q_ref, kn_ref, kr_ref, v_ref, o_ref, kcat_ref, *, tk):
    @pl.when(pl.program_id(2) == 0)
    def _():
        kcat_ref[:, :MLA_NOPE] = kn_ref[...]
        kcat_ref[:, MLA_NOPE:] = kr_ref[...]

    q = q_ref[...]
    tq = q.shape[0]
    nk = kcat_ref.shape[0] // tk

    def body(c, carry):
        m, l, acc = carry
        off = pl.multiple_of(c * tk, tk)
        k = kcat_ref[pl.ds(off, tk), :]
        v = v_ref[pl.ds(off, tk), :]
        s = lax.dot_general(q, k, (((1,), (1,)), ((), ())), preferred_element_type=F32)
        m_new = jnp.maximum(m, jnp.max(s, axis=-1, keepdims=True))
        alpha = jnp.exp(m - m_new)
        p = jnp.exp(s - m_new)
        l = alpha * l + jnp.sum(p, axis=-1, keepdims=True)
        acc = alpha * acc + _dot(p.astype(BF16), v)
        return m_new, l, acc

    init = (jnp.full((tq, 1), -jnp.inf, F32), jnp.zeros((tq, 1), F32), jnp.zeros((tq, MLA_V), F32))
    _, l, acc = lax.fori_loop(0, nk, body, init)
    o_ref[...] = (acc / l).astype(o_ref.dtype)


def _flash_attention(q, kn, kr, v, batch, seq, tq=512, tk=512):
    t = q.shape[0]
    tq, tk = min(tq, seq), min(tk, seq)
    nq = seq // tq
    return pl.pallas_call(
        functools.partial(_flash_kernel, tk=tk),
        out_shape=jax.ShapeDtypeStruct((t, MLA_HEADS * MLA_V), BF16),
        grid=(batch, MLA_HEADS, nq),
        in_specs=[
            pl.BlockSpec((tq, MLA_QK_PAD), lambda b, hh, i: (b * nq + i, hh)),
            pl.BlockSpec((seq, MLA_NOPE), lambda b, hh, i: (b, hh)),
            pl.BlockSpec((seq, LANES), lambda b, hh, i: (b, 0)),
            pl.BlockSpec((seq, MLA_V), lambda b, hh, i: (b, hh)),
        ],
        out_specs=pl.BlockSpec((tq, MLA_V), lambda b, hh, i: (b * nq + i, hh)),
        scratch_shapes=[pltpu.VMEM((seq, MLA_QK_PAD), BF16)],
        compiler_params=_params("parallel", "parallel", "arbitrary"),
        name="mla_flash_attention",
    )(q, kn, kr, v)


def _mla_layer(h, hn, w_in, q_norm, kv_norm, w_uq, w_ukv, w_o, next_gain, batch, seq):
    nh, nope, rp, vd = MLA_HEADS, MLA_NOPE, MLA_ROPE, MLA_V
    lat = MLA_Q_LORA + MLA_KV_LORA
    tm = min(1024, seq)
    n_pos = seq // tm
    cos, sin = _rope_tables(seq, rp)
    cos2, sin2 = jnp.concatenate([cos, cos], 1), jnp.concatenate([sin, sin], 1)
    pos_spec = lambda width: pl.BlockSpec((tm, width), lambda i, j: (i % n_pos, 0))

    gains = jnp.concatenate([q_norm, kv_norm]).reshape(1, lat)
    c = _mm(hn, [w_in[:, :lat].astype(BF16)], _epi_tile_rmsnorm, tm=tm, tn=MLA_Q_LORA,
            extras=(gains,), extra_specs=(pl.BlockSpec((1, MLA_Q_LORA), lambda i, j: (0, j)),),
            name="mla_latent_proj")
    c_q, c_kv = c[:, :MLA_Q_LORA], c[:, MLA_Q_LORA:]

    w_kr = w_in[:, lat:]
    pad = lambda w: jnp.pad(w, ((0, 0), (0, LANES - rp))).astype(BF16)
    zeros = jnp.zeros((seq, LANES - rp), F32)
    kr = _mm(hn, [pad(w_kr), pad(_rotate_half_cols(w_kr, rp))], _epi_cos_sin, tm=tm, tn=LANES,
             extras=(jnp.concatenate([cos2, zeros], 1), jnp.concatenate([sin2, zeros], 1)),
             extra_specs=(pos_spec(LANES), pos_spec(LANES)), name="mla_rope_key_proj")

    scale = (nope + rp) ** -0.5
    wq = w_uq.reshape(MLA_Q_LORA, nh, nope + rp)
    zq = jnp.zeros((MLA_Q_LORA, nh, MLA_QK_PAD - nope - rp), F32)
    w_a = jnp.concatenate([wq, zq], axis=2).reshape(MLA_Q_LORA, nh * MLA_QK_PAD).astype(BF16)
    wq_rot = _rotate_half_cols(wq[:, :, nope:].reshape(MLA_Q_LORA, nh * rp), rp).reshape(MLA_Q_LORA, nh, rp)
    w_b = jnp.concatenate([jnp.zeros((MLA_Q_LORA, nh, nope), F32), wq_rot, zq], axis=2)
    w_b = w_b.reshape(MLA_Q_LORA, nh * MLA_QK_PAD).astype(BF16)
    zpad = jnp.zeros((seq, MLA_QK_PAD - nope - rp), F32)
    cos_q = jnp.concatenate([jnp.ones((seq, nope), F32), cos2, zpad], 1) * scale
    sin_q = jnp.concatenate([jnp.zeros((seq, nope), F32), sin2, zpad], 1) * scale
    q_tn = 2 * MLA_QK_PAD
    q = _mm(c_q, [w_a, w_b], _epi_cos_sin, tm=tm, tn=q_tn,
            extras=(jnp.tile(cos_q, (1, 2)), jnp.tile(sin_q, (1, 2))),
            extra_specs=(pos_spec(q_tn), pos_spec(q_tn)), name="mla_query_proj")

    wkv = w_ukv.reshape(MLA_KV_LORA, nh, nope + vd)
    kn = _mm(c_kv, [wkv[:, :, :nope].reshape(MLA_KV_LORA, nh * nope).astype(BF16)], _epi_plain,
             tm=tm, tn=512, name="mla_key_proj")
    v = _mm(c_kv, [wkv[:, :, nope:].reshape(MLA_KV_LORA, nh * vd).astype(BF16)], _epi_plain,
            tm=tm, tn=512, name="mla_value_proj")

    o = _flash_attention(q, kn, kr, v, batch, seq)
    return _mm_res_norm(o, w_o.astype(BF16), h, next_gain, name="mla_out_proj")


def _dense_ffn(h, hn, w_gate, w_up, w_down, next_gain):
    u = _mm(hn, [w_gate.astype(BF16), w_up.astype(BF16)], _epi_swiglu, tm=1024, tn=512, name="ffn_gate_up")
    return _mm_res_norm(u, w_down.astype(BF16), h, next_gain, name="ffn_down")


def _router_kernel(x_ref, w_ref, e_ref, p_ref):
    logits = _dot(x_ref[...], w_ref[...])
    lane = lax.broadcasted_iota(jnp.int32, logits.shape, 1).astype(F32)
    neg = jnp.float32(-jnp.inf)
    lg = jnp.where(lane < N_EXPERTS, logits, neg)
    m1 = jnp.max(lg, axis=-1, keepdims=True)
    i1 = jnp.min(jnp.where(lg == m1, lane, float(LANES)), axis=-1, keepdims=True)
    lg2 = jnp.where(lane == i1, neg, lg)
    m2 = jnp.max(lg2, axis=-1, keepdims=True)
    i2 = jnp.min(jnp.where(lg2 == m2, lane, float(LANES)), axis=-1, keepdims=True)
    e = jnp.exp(m2 - m1)
    w1 = 1.0 / (1.0 + e)
    w2 = e / (1.0 + e)
    e_ref[...] = jnp.where(lane == 0.0, i1, i2).astype(jnp.int32)
    p_ref[...] = jnp.where(lane == 0.0, w1, w2)


def _router(hn, w_router, tm=1024):
    m, d = hn.shape
    tm = min(tm, m)
    w = jnp.pad(w_router, ((0, 0), (0, LANES - N_EXPERTS))).astype(BF16)
    return pl.pallas_call(
        _router_kernel,
        out_shape=(jax.ShapeDtypeStruct((m, LANES), jnp.int32), jax.ShapeDtypeStruct((m, LANES), F32)),
        grid=(m // tm,),
        in_specs=[pl.BlockSpec((tm, d), lambda i: (i, 0)), pl.BlockSpec((d, LANES), lambda i: (0, 0))],
        out_specs=(pl.BlockSpec((tm, LANES), lambda i: (i, 0)), pl.BlockSpec((tm, LANES), lambda i: (i, 0))),
        compiler_params=_params("parallel"),
        name="moe_router",
    )(hn, w)


def _row_copy(src_ref, dst_ref, sem, n_rows):
    return pltpu.make_async_copy(src_ref.at[pl.ds(0, n_rows)], dst_ref.at[pl.ds(0, n_rows)], sem)


def _dispatch_kernel(dest_ref, x_ref, xs_in_ref, xs_ref, sem, *, tb):
    del xs_in_ref
    base = pl.program_id(0) * (tb * TOP_K)

    def issue(t, carry):
        for kk in range(TOP_K):
            pltpu.make_async_copy(x_ref.at[t], xs_ref.at[dest_ref[base + TOP_K * t + kk]], sem).start()
        return carry

    lax.fori_loop(0, tb, issue, 0)
    for _ in range(TOP_K):
        _row_copy(x_ref, xs_ref, sem, tb).wait()


def _dispatch(hn3, dest, n_rows, tb=512):
    t, s, l = hn3.shape
    tb = min(tb, t)
    return pl.pallas_call(
        functools.partial(_dispatch_kernel, tb=tb),
        out_shape=jax.ShapeDtypeStruct((n_rows, s, l), hn3.dtype),
        grid_spec=pltpu.PrefetchScalarGridSpec(
            num_scalar_prefetch=1,
            grid=(t // tb,),
            in_specs=[pl.BlockSpec((tb, s, l), lambda i, d: (i, 0, 0)), pl.BlockSpec(memory_space=pl.ANY)],
            out_specs=pl.BlockSpec(memory_space=pl.ANY),
            scratch_shapes=[pltpu.SemaphoreType.DMA(())],
        ),
        input_output_aliases={2: 0},
        compiler_params=_params("arbitrary"),
        name="moe_dispatch",
    )(dest, hn3, jnp.zeros((n_rows, s, l), hn3.dtype))


def _combine_kernel(dest_ref, y_ref, w_ref, o_ref, buf_ref, sem, *, tb):
    base = pl.program_id(0) * (tb * TOP_K)

    def issue(t, carry):
        for kk in range(TOP_K):
            pltpu.make_async_copy(y_ref.at[dest_ref[base + TOP_K * t + kk]], buf_ref.at[kk * tb + t], sem).start()
        return carry

    lax.fori_loop(0, tb, issue, 0)
    _row_copy(y_ref, buf_ref, sem, TOP_K * tb).wait()
    w = w_ref[...]
    acc = buf_ref[pl.ds(0, tb)].astype(F32) * w[:, :, 0:1]
    for kk in range(1, TOP_K):
        acc = acc + buf_ref[pl.ds(kk * tb, tb)].astype(F32) * w[:, :, kk:kk + 1]
    o_ref[...] = acc.astype(o_ref.dtype)


def _combine(y3, dest, top_w, n_tok, tb=256):
    n_rows, s, l = y3.shape
    tb = min(tb, n_tok)
    w3 = top_w.reshape(n_tok, 1, LANES)
    return pl.pallas_call(
        functools.partial(_combine_kernel, tb=tb),
        out_shape=jax.ShapeDtypeStruct((n_tok, s, l), BF16),
        grid_spec=pltpu.PrefetchScalarGridSpec(
            num_scalar_prefetch=1,
            grid=(n_tok // tb,),
            in_specs=[pl.BlockSpec(memory_space=pl.ANY), pl.BlockSpec((tb, 1, LANES), lambda i, d: (i, 0, 0))],
            out_specs=pl.BlockSpec((tb, s, l), lambda i, d: (i, 0, 0)),
            scratch_shapes=[pltpu.VMEM((TOP_K * tb, s, l), y3.dtype), pltpu.SemaphoreType.DMA(())],
        ),
        compiler_params=_params("arbitrary"),
        name="moe_combine",
    )(dest, y3, w3)


def _gmm_kernel(be_ref, *refs, n_w, epilogue):
    del be_ref
    x = refs[0][...]
    accs = [_dot(x, w[0]) for w in refs[1:1 + n_w]]
    epilogue(accs, (), refs[1 + n_w:])


def _grouped_mm(xs, ws, blk_expert, epilogue, *, tm, tn, name):
    m, k = xs.shape
    n = ws[0].shape[2]
    tn = min(tn, n)
    return pl.pallas_call(
        functools.partial(_gmm_kernel, n_w=len(ws), epilogue=epilogue),
        out_shape=jax.ShapeDtypeStruct((m, n), BF16),
        grid_spec=pltpu.PrefetchScalarGridSpec(
            num_scalar_prefetch=1,
            grid=(n // tn, m // tm),
            in_specs=[pl.BlockSpec((tm, k), lambda j, i, be: (i, 0))]
            + [pl.BlockSpec((1, k, tn), lambda j, i, be: (be[i], 0, j))] * len(ws),
            out_specs=pl.BlockSpec((tm, tn), lambda j, i, be: (i, j)),
        ),
        compiler_params=_params("parallel", "arbitrary"),
        name=name,
    )(blk_expert, xs, *ws)


def _moe_ffn(hn, w_router, w_gate, w_up, w_down, tm=512):
    n_tok, d = hn.shape
    n_asg = n_tok * TOP_K
    tm = min(tm, n_tok)
    top_e, top_w = _router(hn, w_router)

    flat_e = top_e[:, :TOP_K].reshape(n_asg)
    onehot = (jnp.arange(N_EXPERTS, dtype=jnp.int32)[:, None] == flat_e[None, :]).astype(jnp.int32)
    csum = jnp.cumsum(onehot, axis=1)
    counts = csum[:, -1]
    padded = (counts + tm - 1) // tm * tm
    pad_end = jnp.cumsum(padded)
    pad_start = pad_end - padded
    dest = jnp.sum(onehot * (csum - onehot + pad_start[:, None]), axis=0).astype(jnp.int32)
    n_blocks = n_asg // tm + N_EXPERTS
    n_rows = n_blocks * tm
    blk_expert = jnp.minimum(
        jnp.searchsorted(pad_end, jnp.arange(n_blocks, dtype=jnp.int32) * tm, side="right"),
        N_EXPERTS - 1).astype(jnp.int32)

    s = d // LANES
    xs = _dispatch(hn.reshape(n_tok, s, LANES), dest, n_rows).reshape(n_rows, d)
    u = _grouped_mm(xs, [w_gate.astype(BF16), w_up.astype(BF16)], blk_expert, _epi_swiglu,
                    tm=tm, tn=1408, name="moe_gate_up")
    y = _grouped_mm(u, [w_down.astype(BF16)], blk_expert, _epi_plain, tm=tm, tn=1024, name="moe_down")
    ff = _combine(y.reshape(n_rows, s, LANES), dest, top_w, n_tok)
    return ff.reshape(n_tok, d)


def kernel(x, p, mix_norm, ffn_norm, ple_norm, ret_w_in, ret_log_decay, ret_w_out, mla_w_in, mla_q_norm,
           mla_kv_norm, mla_w_uq, mla_w_ukv, mla_w_o, ffn_w_gate, ffn_w_up, ffn_w_down, moe_w_router,
           moe_w_gate, moe_w_up, moe_w_down, ple_w_up, ple_w_gate, final_norm):
    batch, seq, d = x.shape
    depth = p.shape[0]
    t = batch * seq
    h = x.reshape(t, d)
    hn = _rmsnorm(h, mix_norm[0], BF16)
    for i in range(depth):
        j = i // 2
        last = i == depth - 1
        next_gain = final_norm if last else mix_norm[i + 1]
        p_i = p[i].reshape(t, -1)
        w_pg, w_pu = ple_w_gate[i].astype(BF16), ple_w_up[i].astype(BF16)
        if i % 2 == 0:
            h, hn = _retention_layer(h, hn, ret_w_in[j], ret_log_decay[j], ret_w_out[j], ffn_norm[i], batch, seq)
            h, hn = _dense_ffn(h, hn, ffn_w_gate[j], ffn_w_up[j], ffn_w_down[j], ple_norm[i])
            h, hn = _ple(h, hn, None, w_pg, p_i, w_pu, next_gain, add_ff=False, emit_h=not last)
        else:
            h, hn = _mla_layer(h, hn, mla_w_in[j], mla_q_norm[j], mla_kv_norm[j], mla_w_uq[j], mla_w_ukv[j],
                               mla_w_o[j], ffn_norm[i], batch, seq)
            ff = _moe_ffn(hn, moe_w_router[j], moe_w_gate[j], moe_w_up[j], moe_w_down[j])
            h, hn = _ple(h, ff, ple_norm[i], w_pg, p_i, w_pu, next_gain, add_ff=True, emit_h=not last)
    return hn.reshape(batch, seq, d)
```
